```python
import math
import jax
import jax.numpy as jnp
from jax import lax
import numpy as np

D_MODEL = 1024
BATCH = 8
SEQ = 4096
DEPTH = 4

GRID_W = 64
CTX_LEN = 256
N_MIXERS = 3
FFN_HIDDEN = -(-8 * D_MODEL // (3 * 256)) * 256
CONV_WIDTH = 3
MLA_HEADS = D_MODEL // 64
MLA_Q_LORA = 3 * D_MODEL // 4
MLA_KV_LORA = D_MODEL // 4
MLA_NOPE = 64
MLA_ROPE = 32
MLA_V = 64
DIFF_HEADS = 8
DIFF_HEAD_DIM = D_MODEL // (2 * DIFF_HEADS)
ROPE_THETA = 10000.0
NORM_EPS = 1e-6
Q_BLOCK = 128

kernel_name = 'hybrid_conv_mla_diffattn_dit_trunk'


def rms_norm(x, g):
    xf = x.astype(jnp.float32)
    y = xf * lax.rsqrt(jnp.mean(xf * xf, axis=-1, keepdims=True) + NORM_EPS)
    return (y * g.astype(jnp.float32)).astype(x.dtype)


def modulate(h, shift, scale):
    return h * (1.0 + scale) + shift


def swiglu(h, w_gate, w_up, w_down):
    return (jax.nn.silu(h @ w_gate) * (h @ w_up)) @ w_down


def axial_rope_tables(rows, rot_dim):
    axis_dim = rot_dim // 2
    inv_freq = ROPE_THETA ** (-jnp.arange(0, axis_dim, 2, dtype=jnp.float32) / axis_dim)
    row = jnp.repeat(jnp.arange(rows, dtype=jnp.float32), GRID_W)
    col = jnp.tile(jnp.arange(GRID_W, dtype=jnp.float32), rows)
    ang_r = row[:, None] * inv_freq
    ang_c = col[:, None] * inv_freq
    ang = jnp.concatenate([ang_r, ang_r, ang_c, ang_c], axis=-1)
    return jnp.cos(ang), jnp.sin(ang)


def apply_axial_rope(x, cos, sin):
    a = x.shape[-1] // 2
    h = a // 2
    xr, xc = x[..., :a], x[..., a:]
    rot = jnp.concatenate([-xr[..., h:], xr[..., :h], -xc[..., h:], xc[..., :h]], axis=-1)
    return x * cos.astype(x.dtype) + rot * sin.astype(x.dtype)


def sweep_query_blocks(block_fn, *qs):
    b, n = qs[0].shape[:2]
    if n % Q_BLOCK or n == Q_BLOCK:
        return block_fn(*qs)
    nb = n // Q_BLOCK
    blocks = tuple(jnp.swapaxes(a.reshape(b, nb, Q_BLOCK, *a.shape[2:]), 0, 1) for a in qs)
    out = lax.map(lambda blk: block_fn(*blk), blocks)
    out = jnp.swapaxes(out, 0, 1)
    return out.reshape(b, n, *out.shape[3:])


def short_conv_mixer(h, w_in, conv_k, conv_b, w_out):
    b_gate, c_gate, u = jnp.split(h @ w_in, 3, axis=-1)
    z = c_gate * u
    pad = CONV_WIDTH // 2
    y = lax.conv_general_dilated(z, conv_k[:, None, :], window_strides=(1,), padding=((pad, pad),),
                                 dimension_numbers=('NWC', 'WIO', 'NWC'), feature_group_count=D_MODEL)
    y = y + conv_b
    return (b_gate * y) @ w_out


def _mla_project(h, cos, sin, with_q, w_in, q_norm_g, w_q_up, kv_norm_g, w_kv_up,
                 qn_nope_g, qn_rope_g, kn_nope_g, kn_rope_g):
    b, n, _ = h.shape
    proj = h @ (w_in if with_q else w_in[:, MLA_Q_LORA:])
    if with_q:
        q_lat, proj = proj[..., :MLA_Q_LORA], proj[..., MLA_Q_LORA:]
    kv_lat, k_pe = proj[..., :MLA_KV_LORA], proj[..., MLA_KV_LORA:]
    kv = (rms_norm(kv_lat, kv_norm_g) @ w_kv_up).reshape(b, n, MLA_HEADS, MLA_NOPE + MLA_V)
    k_nope = rms_norm(kv[..., :MLA_NOPE], kn_nope_g)
    v = kv[..., MLA_NOPE:]
    k_pe = rms_norm(k_pe, kn_rope_g)
    if cos is not None:
        k_pe = apply_axial_rope(k_pe, cos, sin)
    if not with_q:
        return None, None, k_nope, k_pe, v
    q = (rms_norm(q_lat, q_norm_g) @ w_q_up).reshape(b, n, MLA_HEADS, MLA_NOPE + MLA_ROPE)
    q_nope = rms_norm(q[..., :MLA_NOPE], qn_nope_g)
    q_pe = rms_norm(q[..., MLA_NOPE:], qn_rope_g)
    if cos is not None:
        q_pe = apply_axial_rope(q_pe, cos[:, None, :], sin[:, None, :])
    return q_nope, q_pe, k_nope, k_pe, v


def _mla_block(k_nope, k_pe, v):
    scale = (MLA_NOPE + MLA_ROPE) ** -0.5

    def attend(q_nope, q_pe):
        s = (jnp.einsum('bqhd,bkhd->bhqk', q_nope, k_nope)
             + jnp.einsum('bqhr,bkr->bhqk', q_pe, k_pe))
        p = jax.nn.softmax(s.astype(jnp.float32) * scale, axis=-1).astype(v.dtype)
        return jnp.einsum('bhqk,bkhd->bqhd', p, v)
    return attend


def mla_mixer(a_ctx, a_lat, ctx_out, cos, sin, w_in, q_norm_g, w_q_up, kv_norm_g, w_kv_up,
              qn_nope_g, qn_rope_g, kn_nope_g, kn_rope_g, w_out):
    prm = (w_in, q_norm_g, w_q_up, kv_norm_g, w_kv_up, qn_nope_g, qn_rope_g, kn_nope_g, kn_rope_g)
    cq_n, cq_p, ck_n, ck_p, cv = _mla_project(a_ctx, None, None, ctx_out, *prm)
    lq_n, lq_p, lk_n, lk_p, lv = _mla_project(a_lat, cos, sin, True, *prm)
    b, n = a_lat.shape[:2]
    attend_lat = _mla_block(jnp.concatenate([ck_n, lk_n], axis=1),
                            jnp.concatenate([ck_p, lk_p], axis=1),
                            jnp.concatenate([cv, lv], axis=1))
    o_lat = sweep_query_blocks(attend_lat, lq_n, lq_p).reshape(b, n, MLA_HEADS * MLA_V) @ w_out
    o_ctx = None
    if ctx_out:
        o_ctx = sweep_query_blocks(_mla_block(ck_n, ck_p, cv), cq_n, cq_p)
        o_ctx = o_ctx.reshape(b, a_ctx.shape[1], MLA_HEADS * MLA_V) @ w_out
    return o_ctx, o_lat


def _diff_project(h, cos, sin, with_q, w_in, qn_g, kn_g):
    b, n, _ = h.shape
    proj = h @ (w_in if with_q else w_in[:, D_MODEL:])
    parts = jnp.split(proj, 3 if with_q else 2, axis=-1)
    k = rms_norm(parts[-2].reshape(b, n, DIFF_HEADS, 2, DIFF_HEAD_DIM), kn_g)
    v = parts[-1].reshape(b, n, DIFF_HEADS, 2 * DIFF_HEAD_DIM)
    if cos is not None:
        k = apply_axial_rope(k, cos[:, None, None, :], sin[:, None, None, :])
    if not with_q:
        return None, k, v
    q = rms_norm(parts[0].reshape(b, n, DIFF_HEADS, 2, DIFF_HEAD_DIM), qn_g)
    if cos is not None:
        q = apply_axial_rope(q, cos[:, None, None, :], sin[:, None, None, :])
    return q, k, v


def _diff_block(k, v, lam):
    scale = DIFF_HEAD_DIM ** -0.5
    k1, k2 = k[..., 0, :], k[..., 1, :]

    def attend(q):
        s1 = jnp.einsum('bqhd,bkhd->bhqk', q[..., 0, :], k1).astype(jnp.float32) * scale
        s2 = jnp.einsum('bqhd,bkhd->bhqk', q[..., 1, :], k2).astype(jnp.float32) * scale
        p = jax.nn.softmax(s1, axis=-1) - lam * jax.nn.softmax(s2, axis=-1)
        return jnp.einsum('bhqk,bkhe->bqhe', p.astype(v.dtype), v)
    return attend


def diff_mixer(a_ctx, a_lat, ctx_out, cos, sin, lam_init, w_in, qn_g, kn_g,
               lq1, lk1, lq2, lk2, sub_g, w_out):
    lam = (jnp.exp(jnp.sum(lq1.astype(jnp.float32) * lk1.astype(jnp.float32)))
           - jnp.exp(jnp.sum(lq2.astype(jnp.float32) * lk2.astype(jnp.float32))) + lam_init)
    cq, ck, cv = _diff_project(a_ctx, None, None, ctx_out, w_in, qn_g, kn_g)
    lq, lk, lv = _diff_project(a_lat, cos, sin, True, w_in, qn_g, kn_g)

    def finish(o):
        o = rms_norm(o, sub_g) * (1.0 - lam_init)
        return o.reshape(o.shape[0], o.shape[1], D_MODEL) @ w_out

    attend_lat = _diff_block(jnp.concatenate([ck, lk], axis=1), jnp.concatenate([cv, lv], axis=1), lam)
    o_lat = finish(sweep_query_blocks(attend_lat, lq))
    o_ctx = finish(sweep_query_blocks(_diff_block(ck, cv, lam), cq)) if ctx_out else None
    return o_ctx, o_lat


def setup_inputs(seed: int = 0) -> dict:
    key = jax.random.key(seed)
    ks = iter(jax.random.split(key, 40))

    def nrm(shape, scale):
        return jax.random.normal(next(ks), shape, jnp.float32) * scale

    def gain(shape):
        return 1.0 + 0.02 * jax.random.normal(next(ks), shape, jnp.float32)

    n_a, n_b, n_c = (len(range(k, DEPTH, N_MIXERS)) for k in range(N_MIXERS))
    d, f = D_MODEL, FFN_HIDDEN
    return {
        'x': nrm((BATCH, SEQ, d), 1.0),
        'c': nrm((BATCH, d), 1.0),
        'ctx': nrm((BATCH, CTX_LEN, d), 1.0),
        'c_ctx': nrm((d,), 1.0),
        'norm_mix_g': gain((DEPTH, d)),
        'norm_ffn_g': gain((DEPTH, d)),
        'mod_w': nrm((DEPTH, d, 6 * d), 0.5 * d ** -0.5),
        'mod_b': nrm((DEPTH, 6 * d), 0.02),
        'ffn_w_gate': nrm((DEPTH, d, f), d ** -0.5),
        'ffn_w_up': nrm((DEPTH, d, f), d ** -0.5),
        'ffn_w_down': nrm((DEPTH, f, d), f ** -0.5),
        'conv_w_in': nrm((n_a, d, 3 * d), d ** -0.5),
        'conv_k': nrm((n_a, CONV_WIDTH, d), CONV_WIDTH ** -0.5),
        'conv_b': nrm((n_a, d), 0.02),
        'conv_w_out': nrm((n_a, d, d), d ** -0.5),
        'mla_w_in': nrm((n_b, d, MLA_Q_LORA + MLA_KV_LORA + MLA_ROPE), d ** -0.5),
        'mla_q_norm_g': gain((n_b, MLA_Q_LORA)),
        'mla_w_q_up': nrm((n_b, MLA_Q_LORA, MLA_HEADS * (MLA_NOPE + MLA_ROPE)), MLA_Q_LORA ** -0.5),
        'mla_kv_norm_g': gain((n_b, MLA_KV_LORA)),
        'mla_w_kv_up': nrm((n_b, MLA_KV_LORA, MLA_HEADS * (MLA_NOPE + MLA_V)), MLA_KV_LORA ** -0.5),
        'mla_qn_nope_g': gain((n_b, MLA_NOPE)),
        'mla_qn_rope_g': gain((n_b, MLA_ROPE)),
        'mla_kn_nope_g': gain((n_b, MLA_NOPE)),
        'mla_kn_rope_g': gain((n_b, MLA_ROPE)),
        'mla_w_out': nrm((n_b, MLA_HEADS * MLA_V, d), (MLA_HEADS * MLA_V) ** -0.5),
        'diff_w_in': nrm((n_c, d, 3 * d), d ** -0.5),
        'diff_qn_g': gain((n_c, DIFF_HEAD_DIM)),
        'diff_kn_g': gain((n_c, DIFF_HEAD_DIM)),
        'diff_lq1': nrm((n_c, DIFF_HEAD_DIM), 0.1),
        'diff_lk1': nrm((n_c, DIFF_HEAD_DIM), 0.1),
        'diff_lq2': nrm((n_c, DIFF_HEAD_DIM), 0.1),
        'diff_lk2': nrm((n_c, DIFF_HEAD_DIM), 0.1),
        'diff_sub_g': gain((n_c, 2 * DIFF_HEAD_DIM)),
        'diff_w_out': nrm((n_c, d, d), d ** -0.5),
    }


def reference(x, c, ctx, c_ctx, norm_mix_g, norm_ffn_g, mod_w, mod_b, ffn_w_gate, ffn_w_up, ffn_w_down,
              conv_w_in, conv_k, conv_b, conv_w_out,
              mla_w_in, mla_q_norm_g, mla_w_q_up, mla_kv_norm_g, mla_w_kv_up,
              mla_qn_nope_g, mla_qn_rope_g, mla_kn_nope_g, mla_kn_rope_g, mla_w_out,
              diff_w_in, diff_qn_g, diff_kn_g, diff_lq1, diff_lk1, diff_lq2, diff_lk2, diff_sub_g, diff_w_out):
    n_lat = x.shape[1]
    rows = n_lat // GRID_W
    cos_mla, sin_mla = axial_rope_tables(rows, MLA_ROPE)
    cos_diff, sin_diff = axial_rope_tables(rows, DIFF_HEAD_DIM)
    silu_c = jax.nn.silu(c)
    silu_cctx = jax.nn.silu(c_ctx)
    h_lat, h_ctx = x, ctx
    for i in range(DEPTH):
        kind, slot = i % N_MIXERS, i // N_MIXERS
        last = i == DEPTH - 1
        sh1, sc1, g1, sh2, sc2, g2 = jnp.split((silu_c @ mod_w[i] + mod_b[i])[:, None, :], 6, axis=-1)
        a_lat = modulate(rms_norm(h_lat, norm_mix_g[i]), sh1, sc1)
        need_ctx_in = not (last and kind == 0)
        if need_ctx_in:
            csh1, csc1, cg1, csh2, csc2, cg2 = jnp.split(silu_cctx @ mod_w[i] + mod_b[i], 6, axis=-1)
            a_ctx = modulate(rms_norm(h_ctx, norm_mix_g[i]), csh1, csc1)
        if kind == 0:
            o_lat = short_conv_mixer(a_lat, conv_w_in[slot], conv_k[slot], conv_b[slot], conv_w_out[slot])
            o_ctx = None if last else short_conv_mixer(a_ctx, conv_w_in[slot], conv_k[slot], conv_b[slot],
                                                       conv_w_out[slot])
        elif kind == 1:
            o_ctx, o_lat = mla_mixer(a_ctx, a_lat, not last, cos_mla, sin_mla,
                                     mla_w_in[slot], mla_q_norm_g[slot], mla_w_q_up[slot],
                                     mla_kv_norm_g[slot], mla_w_kv_up[slot],
                                     mla_qn_nope_g[slot], mla_qn_rope_g[slot],
                                     mla_kn_nope_g[slot], mla_kn_rope_g[slot], mla_w_out[slot])
        else:
            lam_init = 0.8 - 0.6 * math.exp(-0.3 * i)
            o_ctx, o_lat = diff_mixer(a_ctx, a_lat, not last, cos_diff, sin_diff, lam_init,
                                      diff_w_in[slot], diff_qn_g[slot], diff_kn_g[slot],
                                      diff_lq1[slot], diff_lk1[slot], diff_lq2[slot], diff_lk2[slot],
                                      diff_sub_g[slot], diff_w_out[slot])
        h_lat = h_lat + g1 * o_lat
        h_lat = h_lat + g2 * swiglu(modulate(rms_norm(h_lat, norm_ffn_g[i]), sh2, sc2),
                                    ffn_w_gate[i], ffn_w_up[i], ffn_w_down[i])
        if not last:
            h_ctx = h_ctx + cg1 * o_ctx
            h_ctx = h_ctx + cg2 * swiglu(modulate(rms_norm(h_ctx, norm_ffn_g[i]), csh2, csc2),
                                         ffn_w_gate[i], ffn_w_up[i], ffn_w_down[i])
    return h_lat
```

```python
import functools
import math

import jax
import jax.numpy as jnp
from jax import lax
from jax.experimental import pallas as pl
from jax.experimental.pallas import tpu as pltpu

F32 = jnp.float32
BF16 = jnp.bfloat16

NORM_EPS = 1e-6
ROPE_THETA = 10000.0
GRID_W = 64
DEPTH = 4
N_MIXERS = 3
MLA_HEADS = 16
MLA_Q_LORA = 768
MLA_KV_LORA = 256
MLA_NOPE = 64
MLA_ROPE = 32
MLA_V = 64
DIFF_HEADS = 8
DIFF_HEAD_DIM = 64

LANES = 128
ROW_TILE = 256
HALO = 16
FFN_CHUNK = 256
LOG2E = 1.4426950408889634
VMEM_LIMIT = 56 * 1024 * 1024


def _cparams(n_grid):
    return pltpu.CompilerParams(dimension_semantics=("arbitrary",) * n_grid,
                                vmem_limit_bytes=VMEM_LIMIT)


def _full(shape):
    nd = len(shape)
    return pl.BlockSpec(shape, lambda *_: (0,) * nd)


def _rms(x, n):
    return x * lax.rsqrt(jnp.sum(x * x, axis=-1, keepdims=True) * (1.0 / n) + NORM_EPS)


def _norm_mod(h, g, shift, scale):
    return (_rms(h, h.shape[-1]) * g) * (1.0 + scale) + shift


def _dot(a, b):
    return jnp.dot(a, b, preferred_element_type=F32)


def _dot_nt(a, b):
    return lax.dot_general(a, b, (((1,), (1,)), ((), ())), preferred_element_type=F32)


def _silu(x):
    return x * jax.nn.sigmoid(x)


def _mod_kernel(cond_ref, w_ref, b_ref, o_ref):
    x = _silu(cond_ref[...]).astype(BF16)
    o_ref[0, 0] = _dot(x, w_ref[0].astype(BF16)) + b_ref[0]


def _mod_vectors(cond, mod_w, mod_b):
    depth, d, _ = mod_w.shape
    rows = cond.shape[0]
    out = pl.pallas_call(
        _mod_kernel,
        grid=(depth, 6),
        in_specs=[_full((rows, d)),
                  pl.BlockSpec((1, d, d), lambda l, v: (l, 0, v)),
                  pl.BlockSpec((1, 1, d), lambda l, v: (l, 0, v))],
        out_specs=pl.BlockSpec((1, 1, rows, d), lambda l, v: (l, v, 0, 0)),
        out_shape=jax.ShapeDtypeStruct((depth, 6, rows, d), F32),
        compiler_params=_cparams(2),
        name="mod_vectors",
    )(cond, mod_w, mod_b.reshape(depth, 1, 6 * d))
    return jnp.transpose(out, (0, 2, 1, 3))


def _mv_spec(layer, d, n_batch, j0):
    return pl.BlockSpec((None, None, 6, d),
                        lambda b, j: (layer, jnp.where(j + j0 == 0, n_batch, b), 0, 0))


def _conv_kernel(h_ref, hp_ref, hn_ref, mv_ref, g_ref, w_ref, ck_ref, cb_ref, o_ref, z_scr,
                 *, j0, n_tiles):
    d = h_ref.shape[-1]
    r = h_ref.shape[1]
    j = pl.program_id(1) + j0
    sh, sc = mv_ref[0:1, :], mv_ref[1:2, :]
    g = g_ref[...]

    def prep(x):
        return _norm_mod(x, g, sh, sc).astype(BF16)

    a_main = prep(h_ref[0])
    lhs = jnp.concatenate([prep(hp_ref[0]), a_main, prep(hn_ref[0])], axis=0)
    cu = _dot(lhs, w_ref[:, d:])
    z = cu[:, :d] * cu[:, d:]
    row = lax.broadcasted_iota(jnp.int32, (r + 2 * HALO, 1), 0)
    prev_ok = j >= 2
    next_ok = jnp.logical_and(j >= 1, j <= n_tiles - 2)
    valid = jnp.logical_or(jnp.logical_and(row >= HALO, row < HALO + r),
                           jnp.logical_or(jnp.logical_and(row < HALO, prev_ok),
                                          jnp.logical_and(row >= HALO + r, next_ok)))
    z_scr[...] = jnp.where(valid, z, 0.0)
    y = (z_scr[pl.ds(HALO - 1, r), :] * ck_ref[0:1, :]
         + z_scr[pl.ds(HALO, r), :] * ck_ref[1:2, :]
         + z_scr[pl.ds(HALO + 1, r), :] * ck_ref[2:3, :]) + cb_ref[...]
    o_ref[0] = (_dot(a_main, w_ref[:, :d]) * y).astype(BF16)


def _conv_pre(hs, mv, layer, g, w_in, conv_k, conv_b, *, n_batch, j0):
    bsz, t, d = hs.shape
    n_tiles = t // ROW_TILE
    hb = ROW_TILE // HALO
    n_hblk = t // HALO
    return pl.pallas_call(
        functools.partial(_conv_kernel, j0=j0, n_tiles=n_tiles),
        grid=(bsz, n_tiles - j0),
        in_specs=[pl.BlockSpec((1, ROW_TILE, d), lambda b, j: (b, j + j0, 0)),
                  pl.BlockSpec((1, HALO, d), lambda b, j: (b, jnp.maximum((j + j0) * hb - 1, 0), 0)),
                  pl.BlockSpec((1, HALO, d),
                               lambda b, j: (b, jnp.minimum((j + j0 + 1) * hb, n_hblk - 1), 0)),
                  _mv_spec(layer, d, n_batch, j0),
                  _full((1, d)), _full((d, 3 * d)), _full((3, d)), _full((1, d))],
        out_specs=pl.BlockSpec((1, ROW_TILE, d), lambda b, j: (b, j + j0, 0)),
        out_shape=jax.ShapeDtypeStruct((bsz, t, d), BF16),
        scratch_shapes=[pltpu.VMEM((ROW_TILE + 2 * HALO, d), F32)],
        compiler_params=_cparams(2),
        name="conv_pre",
    )(hs, hs, hs, mv, g.reshape(1, d), w_in, conv_k, conv_b.reshape(1, d))


def _post_kernel(h_ref, mix_ref, mv_ref, gf_ref, wo_ref, wg_ref, wu_ref, wd_ref, o_ref, hm_scr):
    f = wg_ref.shape[1]
    g1, sh2, sc2, g2 = mv_ref[2:3, :], mv_ref[3:4, :], mv_ref[4:5, :], mv_ref[5:6, :]
    h1 = h_ref[0] + g1 * _dot(mix_ref[0], wo_ref[...])
    a = _norm_mod(h1, gf_ref[...], sh2, sc2).astype(BF16)
    for c in range(f // FFN_CHUNK):
        cols = slice(c * FFN_CHUNK, (c + 1) * FFN_CHUNK)
        hm_scr[:, cols] = (_silu(_dot(a, wg_ref[:, cols])) * _dot(a, wu_ref[:, cols])).astype(BF16)
    o_ref[0] = h1 + g2 * _dot(hm_scr[...], wd_ref[...])


def _post(hs, mix, mv, layer, gf, w_out, wg, wu, wd, *, n_batch, j0):
    bsz, t, d = hs.shape
    f = wg.shape[1]
    n_tiles = t // ROW_TILE
    return pl.pallas_call(
        _post_kernel,
        grid=(bsz, n_tiles - j0),
        in_specs=[pl.BlockSpec((1, ROW_TILE, d), lambda b, j: (b, j + j0, 0)),
                  pl.BlockSpec((1, ROW_TILE, d), lambda b, j: (b, j + j0, 0)),
                  _mv_spec(layer, d, n_batch, j0),
                  _full((1, d)), _full((d, d)), _full((d, f)), _full((d, f)), _full((f, d))],
        out_specs=pl.BlockSpec((1, ROW_TILE, d), lambda b, j: (b, j, 0)),
        out_shape=jax.ShapeDtypeStruct((bsz, t - j0 * ROW_TILE, d), F32),
        scratch_shapes=[pltpu.VMEM((ROW_TILE, f), BF16)],
        compiler_params=_cparams(2),
        name="post_ffn",
    )(hs, mix, mv, gf.reshape(1, d), w_out, wg, wu, wd)


def _rope(x, cos, sin_up, sin_dn, shift):
    return (x * cos + pltpu.roll(x, LANES - shift, axis=1) * sin_up
            + pltpu.roll(x, shift, axis=1) * sin_dn)


def _mla_proj_kernel(h_ref, mv_ref, g_ref, win_ref, qg_ref, wq_ref, kvg_ref, wkv_ref,
                     gq_ref, gk_ref, gkp_ref, cos_ref, su_ref, sd_ref, q_ref, k_ref, v_ref):
    n_heads = q_ref.shape[1]
    a = _norm_mod(h_ref[0], g_ref[...], mv_ref[0:1, :], mv_ref[1:2, :]).astype(BF16)
    proj = _dot(a, win_ref[...])
    ql, kvl = MLA_Q_LORA, MLA_KV_LORA
    qn = (_rms(proj[:, :ql], ql) * qg_ref[...]).astype(BF16)
    kvn = (_rms(proj[:, ql:ql + kvl], kvl) * kvg_ref[...]).astype(BF16)
    qh = _dot(qn, wq_ref[...])
    kvh = _dot(kvn, wkv_ref[...])
    cos, su, sd = cos_ref[...], su_ref[...], sd_ref[...]
    lane = lax.broadcasted_iota(jnp.int32, (1, LANES), 1)
    is_nope = lane < MLA_NOPE
    is_rope = jnp.logical_and(lane >= MLA_NOPE, lane < MLA_NOPE + MLA_ROPE)
    shift = MLA_ROPE // 4

    kp = proj[:, ql + kvl:]
    kp = _rope(_rms(kp, MLA_ROPE) * gkp_ref[...], cos, su, sd, shift)
    gq, gk = gq_ref[...], gk_ref[...]
    for h in range(n_heads):
        cols = slice(h * LANES, (h + 1) * LANES)
        x = qh[:, cols]
        x2 = x * x
        ms_n = jnp.sum(jnp.where(is_nope, x2, 0.0), axis=-1, keepdims=True) * (1.0 / MLA_NOPE)
        ms_r = jnp.sum(jnp.where(is_rope, x2, 0.0), axis=-1, keepdims=True) * (1.0 / MLA_ROPE)
        rinv = jnp.where(is_nope, lax.rsqrt(ms_n + NORM_EPS), lax.rsqrt(ms_r + NORM_EPS))
        q_ref[0, h] = _rope(x * rinv * gq, cos, su, sd, shift).astype(BF16)
        kx = kvh[:, cols]
        k_ref[0, h] = (_rms(kx, MLA_NOPE) * gk + kp).astype(BF16)
        v_ref[0, h] = kvh[:, n_heads * LANES + h * LANES:n_heads * LANES + (h + 1) * LANES].astype(BF16)


def _mla_proj(hs, mv, layer, g, w, tables, *, n_batch):
    bsz, t, d = hs.shape
    n_tiles = t // ROW_TILE
    hd = MLA_HEADS
    head_spec = pl.BlockSpec((1, hd, ROW_TILE, LANES), lambda b, j: (b, 0, j, 0))
    tab_spec = pl.BlockSpec((ROW_TILE, LANES), lambda b, j: (j, 0))
    out_sd = jax.ShapeDtypeStruct((bsz, hd, t, LANES), BF16)
    return pl.pallas_call(
        _mla_proj_kernel,
        grid=(bsz, n_tiles),
        in_specs=[pl.BlockSpec((1, ROW_TILE, d), lambda b, j: (b, j, 0)),
                  _mv_spec(layer, d, n_batch, 0),
                  _full((1, d)), _full(w["w_in"].shape), _full((1, MLA_Q_LORA)), _full(w["w_q_up"].shape),
                  _full((1, MLA_KV_LORA)), _full(w["w_kv_up"].shape),
                  _full((1, LANES)), _full((1, LANES)), _full((1, LANES)),
                  tab_spec, tab_spec, tab_spec],
        out_specs=[head_spec, head_spec, head_spec],
        out_shape=[out_sd, out_sd, out_sd],
        compiler_params=_cparams(2),
        name="mla_proj",
    )(hs, mv, g.reshape(1, d), w["w_in"], w["q_norm_g"], w["w_q_up"], w["kv_norm_g"], w["w_kv_up"],
      w["gq"], w["gk"], w["gkp"], *tables)


def _mla_attn_kernel(q_ref, k_ref, v_ref, o_ref, *, n_ctx):
    t = q_ref.shape[2]
    r = ROW_TILE

    def attend(q0, n_keys):
        out = None
        for hh in range(q_ref.shape[1]):
            q = q_ref[0, hh, pl.ds(q0, r), :]
            s = _dot_nt(q, k_ref[0, hh, 0:n_keys, :])
            p = jnp.exp2(s - jnp.max(s, axis=-1, keepdims=True))
            l = jnp.sum(p, axis=-1, keepdims=True)
            o = _dot(p.astype(BF16), v_ref[0, hh, 0:n_keys, :]) / l
            out = o if out is None else out + o
        return out.astype(BF16)

    o_ref[0, 0:n_ctx, :] = attend(0, n_ctx)

    def body(i, carry):
        q0 = pl.multiple_of(i * r, r)
        o_ref[0, pl.ds(q0, r), :] = attend(q0, t)
        return carry

    lax.fori_loop(n_ctx // r, t // r, body, 0)


def _mla_attn(q, k, v, *, n_ctx):
    bsz, hd, t, _ = q.shape
    pair = LANES // MLA_V
    spec = pl.BlockSpec((1, pair, t, LANES), lambda b, h: (b, h, 0, 0))
    return pl.pallas_call(
        functools.partial(_mla_attn_kernel, n_ctx=n_ctx),
        grid=(bsz, hd // pair),
        in_specs=[spec, spec, spec],
        out_specs=pl.BlockSpec((1, t, LANES), lambda b, h: (b, 0, h)),
        out_shape=jax.ShapeDtypeStruct((bsz, t, hd * MLA_V), BF16),
        compiler_params=_cparams(2),
        name="mla_attn",
    )(q, k, v)


def _diff_proj_kernel(h_ref, mv_ref, g_ref, win_ref, gq_ref, gk_ref, cos_ref, su_ref, sd_ref,
                      q_ref, k_ref, v_ref):
    n_heads = q_ref.shape[1]
    d = h_ref.shape[-1]
    a = _norm_mod(h_ref[0], g_ref[...], mv_ref[0:1, :], mv_ref[1:2, :]).astype(BF16)
    proj = _dot(a, win_ref[...])
    cos, su, sd = cos_ref[...], su_ref[...], sd_ref[...]
    lane = lax.broadcasted_iota(jnp.int32, (1, LANES), 1)
    first = lane < DIFF_HEAD_DIM
    shift = DIFF_HEAD_DIM // 4

    def norm_rope(x, gain):
        x2 = x * x
        tot = jnp.sum(x2, axis=-1, keepdims=True)
        s1 = jnp.sum(jnp.where(first, x2, 0.0), axis=-1, keepdims=True)
        rinv = jnp.where(first, lax.rsqrt(s1 * (1.0 / DIFF_HEAD_DIM) + NORM_EPS),
                         lax.rsqrt((tot - s1) * (1.0 / DIFF_HEAD_DIM) + NORM_EPS))
        return _rope(x * rinv * gain, cos, su, sd, shift).astype(BF16)

    for h in range(n_heads):
        cols = slice(h * LANES, (h + 1) * LANES)
        q_ref[0, h] = norm_rope(proj[:, cols], gq_ref[...])
        k_ref[0, h] = norm_rope(proj[:, d + h * LANES:d + (h + 1) * LANES], gk_ref[...])
        v_ref[0, h] = proj[:, 2 * d + h * LANES:2 * d + (h + 1) * LANES].astype(BF16)


def _diff_proj(hs, mv, layer, g, w_in, gq, gk, tables, *, n_batch):
    bsz, t, d = hs.shape
    n_tiles = t // ROW_TILE
    hd = DIFF_HEADS
    head_spec = pl.BlockSpec((1, hd, ROW_TILE, LANES), lambda b, j: (b, 0, j, 0))
    tab_spec = pl.BlockSpec((ROW_TILE, LANES), lambda b, j: (j, 0))
    out_sd = jax.ShapeDtypeStruct((bsz, hd, t, LANES), BF16)
    return pl.pallas_call(
        _diff_proj_kernel,
        grid=(bsz, n_tiles),
        in_specs=[pl.BlockSpec((1, ROW_TILE, d), lambda b, j: (b, j, 0)),
                  _mv_spec(layer, d, n_batch, 0),
                  _full((1, d)), _full((d, 3 * d)), _full((1, LANES)), _full((1, LANES)),
                  tab_spec, tab_spec, tab_spec],
        out_specs=[head_spec, head_spec, head_spec],
        out_shape=[out_sd, out_sd, out_sd],
        compiler_params=_cparams(2),
        name="diff_proj",
    )(hs, mv, g.reshape(1, d), w_in, gq, gk, *tables)


def _diff_attn_kernel(q_ref, k_ref, v_ref, lam_ref, sg_ref, o_ref, *, n_ctx, lam_init):
    t = q_ref.shape[2]
    r = ROW_TILE
    lv = lam_ref[...]
    lam = (jnp.exp(jnp.sum(lv[0:1] * lv[1:2], axis=-1, keepdims=True))
           - jnp.exp(jnp.sum(lv[2:3] * lv[3:4], axis=-1, keepdims=True)) + lam_init)
    lane = lax.broadcasted_iota(jnp.int32, (1, LANES), 1)
    first = lane < DIFF_HEAD_DIM

    def softmax_parts(q, k):
        s = _dot_nt(q, k)
        p = jnp.exp2(s - jnp.max(s, axis=-1, keepdims=True))
        return p, jnp.sum(p, axis=-1, keepdims=True)

    def attend(q0, n_keys):
        q = q_ref[0, 0, pl.ds(q0, r), :]
        k = k_ref[0, 0, 0:n_keys, :]
        zero = jnp.zeros_like(q)
        p1, l1 = softmax_parts(jnp.where(first, q, zero), k)
        p2, l2 = softmax_parts(jnp.where(first, zero, q), k)
        p = p1 * (1.0 / l1) - p2 * (lam / l2)
        o = _dot(p.astype(BF16), v_ref[0, 0, 0:n_keys, :])
        o = _rms(o, 2 * DIFF_HEAD_DIM) * sg_ref[...]
        return (o * (1.0 - lam_init)).astype(BF16)

    o_ref[0, 0:n_ctx, :] = attend(0, n_ctx)

    def body(i, carry):
        q0 = pl.multiple_of(i * r, r)
        o_ref[0, pl.ds(q0, r), :] = attend(q0, t)
        return carry

    lax.fori_loop(n_ctx // r, t // r, body, 0)


def _diff_attn(q, k, v, lam_vecs, sub_g, *, n_ctx, lam_init):
    bsz, hd, t, _ = q.shape
    spec = pl.BlockSpec((1, 1, t, LANES), lambda b, h: (b, h, 0, 0))
    return pl.pallas_call(
        functools.partial(_diff_attn_kernel, n_ctx=n_ctx, lam_init=lam_init),
        grid=(bsz, hd),
        in_specs=[spec, spec, spec, _full((4, DIFF_HEAD_DIM)), _full((1, LANES))],
        out_specs=pl.BlockSpec((1, t, LANES), lambda b, h: (b, 0, h)),
        out_shape=jax.ShapeDtypeStruct((bsz, t, hd * LANES), BF16),
        compiler_params=_cparams(2),
        name="diff_attn",
    )(q, k, v, lam_vecs, sub_g.reshape(1, LANES))


def _rope_tables(n_ctx, n_lat, rot_dim, lane0, copies):
    axis_dim = rot_dim // 2
    inv_freq = ROPE_THETA ** (-jnp.arange(0, axis_dim, 2, dtype=F32) / axis_dim)
    rows = n_lat // GRID_W
    row = jnp.repeat(jnp.arange(rows, dtype=F32), GRID_W)
    col = jnp.tile(jnp.arange(GRID_W, dtype=F32), rows)
    ang_r = row[:, None] * inv_freq
    ang_c = col[:, None] * inv_freq
    ang = jnp.concatenate([ang_r, ang_r, ang_c, ang_c], axis=-1)
    cos, sin = jnp.cos(ang), jnp.sin(ang)
    lower = (jnp.arange(rot_dim) % (rot_dim // 2)) < (rot_dim // 4)
    sin_up = jnp.where(lower, -sin, 0.0)
    sin_dn = jnp.where(lower, 0.0, sin)

    def place(tab, fill):
        full = jnp.full((n_ctx + n_lat, LANES), fill, F32)
        for c in range(copies):
            full = full.at[n_ctx:, lane0 + c * rot_dim:lane0 + (c + 1) * rot_dim].set(tab)
        return full

    return place(cos, 1.0), place(sin_up, 0.0), place(sin_dn, 0.0)


def _mla_weights(w_in, q_norm_g, w_q_up, kv_norm_g, w_kv_up, qn_nope_g, qn_rope_g, kn_nope_g, kn_rope_g):
    d = w_in.shape[0]
    hd, ql, kvl = MLA_HEADS, MLA_Q_LORA, MLA_KV_LORA
    pad_r = LANES - MLA_NOPE - MLA_ROPE
    kpe = jnp.pad(w_in[:, ql + kvl:], ((0, 0), (MLA_NOPE, pad_r)))
    w_in_r = jnp.concatenate([w_in[:, :ql + kvl], kpe], axis=1)
    wq = jnp.pad(w_q_up.reshape(ql, hd, MLA_NOPE + MLA_ROPE), ((0, 0), (0, 0), (0, pad_r)))
    wkv = w_kv_up.reshape(kvl, hd, MLA_NOPE + MLA_V)
    wk = jnp.pad(wkv[:, :, :MLA_NOPE], ((0, 0), (0, 0), (0, LANES - MLA_NOPE)))
    wv = wkv[:, :, MLA_NOPE:].reshape(kvl, hd // 2, 2, MLA_V)
    zero = jnp.zeros_like(wv[:, :, 0])
    wv = jnp.stack([jnp.concatenate([wv[:, :, 0], zero], axis=-1),
                    jnp.concatenate([zero, wv[:, :, 1]], axis=-1)], axis=2)
    w_kv_r = jnp.concatenate([wk.reshape(kvl, hd * LANES), wv.reshape(kvl, hd * LANES)], axis=1)
    q_scale = (MLA_NOPE + MLA_ROPE) ** -0.5 * LOG2E
    zpad = jnp.zeros((pad_r,), F32)
    return dict(
        w_in=w_in_r.astype(BF16), q_norm_g=q_norm_g.reshape(1, ql),
        w_q_up=wq.reshape(ql, hd * LANES).astype(BF16), kv_norm_g=kv_norm_g.reshape(1, kvl),
        w_kv_up=w_kv_r.astype(BF16),
        gq=(jnp.concatenate([qn_nope_g, qn_rope_g, zpad]) * q_scale).reshape(1, LANES),
        gk=jnp.concatenate([kn_nope_g, jnp.zeros((LANES - MLA_NOPE,), F32)]).reshape(1, LANES),
        gkp=jnp.concatenate([jnp.zeros((MLA_NOPE,), F32), kn_rope_g, zpad]).reshape(1, LANES))


def kernel(x, c, ctx, c_ctx, norm_mix_g, norm_ffn_g, mod_w, mod_b, ffn_w_gate, ffn_w_up, ffn_w_down, conv_w_in, conv_k, conv_b, conv_w_out, mla_w_in, mla_q_norm_g, mla_w_q_up, mla_kv_norm_g, mla_w_kv_up, mla_qn_nope_g, mla_qn_rope_g, mla_kn_nope_g, mla_kn_rope_g, mla_w_out, diff_w_in, diff_qn_g, diff_kn_g, diff_lq1, diff_lk1, diff_lq2, diff_lk2, diff_sub_g, diff_w_out):
    bsz, n_lat, d = x.shape
    n_ctx = ctx.shape[1]
    depth = mod_w.shape[0]
    assert n_ctx == ROW_TILE and n_lat % ROW_TILE == 0 and depth == DEPTH and d == MLA_HEADS * MLA_V
    assert (depth - 1) % N_MIXERS == 0

    cond_rows = 16
    cond = jnp.concatenate([c, c_ctx[None, :], jnp.zeros((cond_rows - bsz - 1, d), F32)], axis=0)
    mv = _mod_vectors(cond, mod_w, mod_b)

    tab_mla = _rope_tables(n_ctx, n_lat, MLA_ROPE, MLA_NOPE, 1)
    tab_diff = _rope_tables(n_ctx, n_lat, DIFF_HEAD_DIM, 0, 2)

    hs = jnp.concatenate([ctx, x], axis=1)
    for i in range(depth):
        kind, slot = i % N_MIXERS, i // N_MIXERS
        last = i == depth - 1
        j0 = 1 if last else 0
        if kind == 0:
            mix = _conv_pre(hs, mv, i, norm_mix_g[i], conv_w_in[slot].astype(BF16), conv_k[slot],
                            conv_b[slot], n_batch=bsz, j0=j0)
            w_out = conv_w_out[slot]
        elif kind == 1:
            w = _mla_weights(mla_w_in[slot], mla_q_norm_g[slot], mla_w_q_up[slot], mla_kv_norm_g[slot],
                             mla_w_kv_up[slot], mla_qn_nope_g[slot], mla_qn_rope_g[slot],
                             mla_kn_nope_g[slot], mla_kn_rope_g[slot])
            q, k, v = _mla_proj(hs, mv, i, norm_mix_g[i], w, tab_mla, n_batch=bsz)
            mix = _mla_attn(q, k, v, n_ctx=n_ctx)
            w_out = mla_w_out[slot]
        else:
            lam_init = 0.8 - 0.6 * math.exp(-0.3 * i)
            q_scale = DIFF_HEAD_DIM ** -0.5 * LOG2E
            gq = (jnp.concatenate([diff_qn_g[slot], diff_qn_g[slot]]) * q_scale).reshape(1, LANES)
            gk = jnp.concatenate([diff_kn_g[slot], diff_kn_g[slot]]).reshape(1, LANES)
            q, k, v = _diff_proj(hs, mv, i, norm_mix_g[i], diff_w_in[slot].astype(BF16), gq, gk, tab_diff,
                                 n_batch=bsz)
            lam_vecs = jnp.stack([diff_lq1[slot], diff_lk1[slot], diff_lq2[slot], diff_lk2[slot]])
            mix = _diff_attn(q, k, v, lam_vecs, diff_sub_g[slot], n_ctx=n_ctx, lam_init=lam_init)
            w_out = diff_w_out[slot]
        hs = _post(hs, mix, mv, i, norm_ffn_g[i], w_out.astype(BF16), ffn_w_gate[i].astype(BF16),
                   ffn_w_up[i].astype(BF16), ffn_w_down[i].astype(BF16), n_batch=bsz, j0=j0)
    return hs
```
